```python
import math
import jax, jax.numpy as jnp
from jax import lax
import numpy as np

D_MODEL = 1024
BATCH = 8
SEQ = 4096
DEPTH = 4

N_MIXERS = 3
EXPAND = 2
D_INNER = EXPAND * D_MODEL
LN_EPS = 1e-5
DEEPNORM_ALPHA = (2.0 * DEPTH) ** 0.25
DEEPNORM_BETA = (8.0 * DEPTH) ** -0.25

HY_SHORT_CONV = 3
HY_EMB_DIM = 33
HY_FILTER_HIDDEN = 64
HY_FAST_DECAY_PCT = 0.3
HY_SLOW_DECAY_PCT = 1.5
HY_DECAY_TARGET = 1e-2
HY_IN_DIM = 4 * D_INNER

SSD_HEADDIM = 64
SSD_HEADS = D_INNER // SSD_HEADDIM
SSD_STATE = 128
SSD_GROUPS = 4
SSD_CONV = 5
SSD_CHUNK = 128
SSD_CONV_DIM = D_INNER + 2 * SSD_GROUPS * SSD_STATE
SSD_IN_DIM = D_INNER + SSD_CONV_DIM + 2 * SSD_HEADS
SSD_DT_MIN = 1e-3
SSD_DT_MAX = 1e-1

HG_HEADS = 16
HG_KEY_DIM = 128
HG_VAL_DIM = D_INNER // HG_HEADS
HG_FDIM = HG_HEADS * HG_KEY_DIM
HG_CHUNK = 64
HG_IN_DIM = 3 * HG_FDIM + 2 * D_INNER

kernel_name = "hybrid_hyena_ssd_hgrn2_deepnorm_encoder"


def layer_norm(x, g, b):
    xf = x.astype(jnp.float32)
    mu = jnp.mean(xf, axis=-1, keepdims=True)
    var = jnp.mean(jnp.square(xf - mu), axis=-1, keepdims=True)
    return (xf - mu) * lax.rsqrt(var + LN_EPS) * g + b


def rms_norm_groups(x, g, n_groups):
    shp = x.shape
    xf = x.astype(jnp.float32).reshape(shp[:-1] + (n_groups, shp[-1] // n_groups))
    xf = xf * lax.rsqrt(jnp.mean(jnp.square(xf), axis=-1, keepdims=True) + LN_EPS)
    return xf.reshape(shp) * g


def centred_depthwise_conv(x, w, b):
    k = w.shape[0]
    pad = k // 2
    L = x.shape[1]
    xp = jnp.pad(x, ((0, 0), (pad, pad), (0, 0)))
    return sum(xp[:, j:j + L] * w[j] for j in range(k)) + b


def hyena_filters(L, w1, b1, w2, b2, w3, b3, freq, w_out):
    t = jnp.linspace(0.0, 1.0, L, dtype=jnp.float32)[:, None]
    bands = (HY_EMB_DIM - 1) // 2
    w = 2.0 * math.pi * jnp.arange(L, dtype=jnp.float32)[:, None] / L
    f = jnp.linspace(1e-4, bands - 1, bands, dtype=jnp.float32)[None, :]
    z = jnp.concatenate([t, jnp.cos(f * w), -jnp.sin(f * w)], axis=-1)
    fr = freq.astype(jnp.float32)
    h = jnp.sin(fr * (z @ w1.astype(jnp.float32) + b1))
    h = jnp.sin(fr * (h @ w2.astype(jnp.float32) + b2))
    h = jnp.sin(fr * (h @ w3.astype(jnp.float32) + b3))
    h = h @ w_out.astype(jnp.float32)
    min_d = math.log(HY_DECAY_TARGET) / HY_FAST_DECAY_PCT
    max_d = math.log(HY_DECAY_TARGET) / HY_SLOW_DECAY_PCT
    deltas = jnp.abs(jnp.linspace(min_d, max_d, D_INNER, dtype=jnp.float32))
    h = h * jnp.exp(-t * jnp.concatenate([deltas, deltas]))
    return h[:, :D_INNER], h[:, D_INNER:]


def bidirectional_fftconv(u, h_fwd, h_bwd, skip):
    L = u.shape[1]
    n = 2 * L
    kfull = jnp.concatenate([h_fwd, jnp.zeros((1, h_fwd.shape[1]), jnp.float32), h_bwd[:0:-1]], axis=0)
    k_f = jnp.fft.rfft(kfull, n=n, axis=0)
    uf32 = u.astype(jnp.float32)
    u_f = jnp.fft.rfft(uf32, n=n, axis=1)
    y = jnp.fft.irfft(u_f * k_f[None], n=n, axis=1)[:, :L]
    return y + uf32 * skip.astype(jnp.float32)


def hyena_mixer(x, w_in, conv_w, conv_b, filt_w1, filt_b1, filt_w2, filt_b2, filt_w3, filt_b3,
                filt_freq, filt_w_out, skip, w_out):
    L = x.shape[1]
    proj = x @ w_in
    u, z = proj[..., :3 * D_INNER], proj[..., 3 * D_INNER:]
    u = centred_depthwise_conv(u, conv_w, conv_b)
    x0, x1, v = jnp.split(u.astype(jnp.float32), 3, axis=-1)
    h_f, h_b = hyena_filters(L, filt_w1, filt_b1, filt_w2, filt_b2, filt_w3, filt_b3, filt_freq, filt_w_out)
    y = x0 * bidirectional_fftconv(x1 * v, h_f, h_b, skip)
    y = y * jax.nn.silu(z.astype(jnp.float32))
    return (y.astype(x.dtype) @ w_out).astype(x.dtype)


def segsum(a):
    T = a.shape[-1]
    cs = jnp.cumsum(a, axis=-1)
    diff = cs[..., :, None] - cs[..., None, :]
    return jnp.where(jnp.tril(jnp.ones((T, T), dtype=bool)), diff, -jnp.inf)


def ssd_chunked(xh, dt, a, bm, cm):
    b, L, h, p = xh.shape
    g, n = bm.shape[2], bm.shape[3]
    hg = h // g
    T = SSD_CHUNK
    nc = L // T
    xdt = (xh * dt[..., None]).reshape(b, nc, T, g, hg, p)
    adt = jnp.moveaxis((dt * a).reshape(b, nc, T, g, hg), 2, -1)
    a_cs = jnp.cumsum(adt, axis=-1)
    bc = bm.reshape(b, nc, T, g, n)
    cc = cm.reshape(b, nc, T, g, n)
    lmat = jnp.exp(segsum(adt))
    cb = jnp.einsum('bctgn,bcsgn->bcgts', cc, bc)
    y_diag = jnp.einsum('bcgts,bcghts,bcsghp->bctghp', cb, lmat, xdt)
    decay_states = jnp.exp(a_cs[..., -1:] - a_cs)
    states = jnp.einsum('bcsgn,bcghs,bcsghp->bcghpn', bc, decay_states, xdt)
    chunk_decay = jnp.pad(jnp.moveaxis(a_cs[..., -1], 1, -1), ((0, 0), (0, 0), (0, 0), (1, 0)))
    dchunk = jnp.exp(segsum(chunk_decay))
    states0 = jnp.concatenate([jnp.zeros_like(states[:, :1]), states], axis=1)
    new_states = jnp.einsum('bghzc,bcghpn->bzghpn', dchunk, states0)
    prev_states = new_states[:, :-1]
    y_off = jnp.einsum('bctgn,bcghpn,bcght->bctghp', cc, prev_states, jnp.exp(a_cs))
    return (y_diag + y_off).reshape(b, L, h, p)


def ssd_mixer(x, w_in, conv_w, conv_b, dt_bias, a_log, d_skip, norm_g, w_out):
    b, L, _ = x.shape
    gn = SSD_GROUPS * SSD_STATE
    proj = x @ w_in
    z = proj[..., :D_INNER].astype(jnp.float32)
    xbc = proj[..., D_INNER:D_INNER + SSD_CONV_DIM]
    dt_raw = proj[..., D_INNER + SSD_CONV_DIM:].astype(jnp.float32).reshape(b, L, 2, SSD_HEADS)
    xbc = jax.nn.silu(centred_depthwise_conv(xbc, conv_w, conv_b).astype(jnp.float32))
    xs = xbc[..., :D_INNER].reshape(b, L, SSD_HEADS, SSD_HEADDIM)
    bm = xbc[..., D_INNER:D_INNER + gn].reshape(b, L, SSD_GROUPS, SSD_STATE)
    cm = xbc[..., D_INNER + gn:].reshape(b, L, SSD_GROUPS, SSD_STATE)
    dt = jax.nn.softplus(dt_raw + dt_bias.astype(jnp.float32))
    a = -jnp.exp(a_log.astype(jnp.float32))
    y_f = ssd_chunked(xs, dt[:, :, 0], a[0], bm, cm)
    y_b = ssd_chunked(xs[:, ::-1], dt[:, ::-1, 1], a[1], bm[:, ::-1], cm[:, ::-1])[:, ::-1]
    y = y_f + y_b + xs * d_skip.astype(jnp.float32)[:, None]
    y = rms_norm_groups(y.reshape(b, L, D_INNER) * jax.nn.silu(z), norm_g, SSD_GROUPS)
    return (y.astype(x.dtype) @ w_out).astype(x.dtype)


def hgrn2_chunked(q, k, log_f, v):
    b, L, h, dk = q.shape
    dv = v.shape[-1]
    T = HG_CHUNK
    nc = L // T

    def to_chunks(a):
        return jnp.moveaxis(a.reshape(b, nc, T, h, a.shape[-1]), 1, 0)

    causal = jnp.tril(jnp.ones((T, T), dtype=bool))[None, :, :, None, None]

    def step(S, inp):
        qt, kt, gt, vt = inp
        G = jnp.cumsum(gt, axis=1)
        o_inter = jnp.einsum('bthk,bhkv->bthv', qt * jnp.exp(G), S)
        diff = G[:, :, None] - G[:, None, :]
        decay = jnp.exp(jnp.where(causal, diff, -jnp.inf))
        att = jnp.einsum('bthk,bshk,btshk->bhts', qt, kt, decay)
        o_intra = jnp.einsum('bhts,bshv->bthv', att, vt)
        g_last = G[:, -1]
        k_dec = kt * jnp.exp(g_last[:, None] - G)
        S_new = jnp.exp(g_last)[..., None] * S + jnp.einsum('bshk,bshv->bhkv', k_dec, vt)
        return S_new, o_inter + o_intra

    S0 = jnp.zeros((b, h, dk, dv), jnp.float32)
    _, o = lax.scan(step, S0, (to_chunks(q), to_chunks(k), to_chunks(log_f), to_chunks(v)))
    return jnp.moveaxis(o, 0, 1).reshape(b, L, h, dv)


def hgrn2_mixer(x, w_in, lb, norm_g, w_out):
    b, L, _ = x.shape
    F = HG_FDIM
    proj = (x @ w_in).astype(jnp.float32)
    q = proj[..., :F].reshape(b, L, HG_HEADS, HG_KEY_DIM)
    f_raw = proj[..., F:3 * F].reshape(b, L, 2, HG_HEADS, HG_KEY_DIM)
    v = proj[..., 3 * F:3 * F + D_INNER].reshape(b, L, HG_HEADS, HG_VAL_DIM)
    z = proj[..., 3 * F + D_INNER:]
    lb = lb.astype(jnp.float32).reshape(2, HG_HEADS, HG_KEY_DIM)
    log_f = jnp.logaddexp(jnp.log(lb), jnp.log1p(-lb) + jax.nn.log_sigmoid(f_raw))
    k = (1.0 - lb) * jax.nn.sigmoid(-f_raw)
    o_f = hgrn2_chunked(q, k[:, :, 0], log_f[:, :, 0], v)
    o_b = hgrn2_chunked(q[:, ::-1], k[:, ::-1, 1], log_f[:, ::-1, 1], v[:, ::-1])[:, ::-1]
    o = rms_norm_groups((o_f + o_b).reshape(b, L, D_INNER), norm_g, HG_HEADS) * jax.nn.silu(z)
    return (o.astype(x.dtype) @ w_out).astype(x.dtype)


def _normal(key, shape, scale):
    return jax.random.normal(key, shape, jnp.float32) * scale


def _ln_params(p, ks):
    return {p + "ln_g": 1.0 + _normal(next(ks), (D_MODEL,), 0.02),
            p + "ln_b": _normal(next(ks), (D_MODEL,), 0.02)}


def _hyena_params(p, ks):
    H = HY_FILTER_HIDDEN
    d = {
        p + "w_in": _normal(next(ks), (D_MODEL, HY_IN_DIM), D_MODEL ** -0.5),
        p + "conv_w": _normal(next(ks), (HY_SHORT_CONV, 3 * D_INNER), HY_SHORT_CONV ** -0.5),
        p + "conv_b": _normal(next(ks), (3 * D_INNER,), 0.02),
        p + "filt_w1": _normal(next(ks), (HY_EMB_DIM, H), HY_EMB_DIM ** -0.5),
        p + "filt_b1": _normal(next(ks), (H,), 0.02),
        p + "filt_w2": _normal(next(ks), (H, H), H ** -0.5),
        p + "filt_b2": _normal(next(ks), (H,), 0.02),
        p + "filt_w3": _normal(next(ks), (H, H), H ** -0.5),
        p + "filt_b3": _normal(next(ks), (H,), 0.02),
        p + "filt_freq": 1.0 + _normal(next(ks), (H,), 0.02),
        p + "filt_w_out": _normal(next(ks), (H, 2 * D_INNER), H ** -0.5),
        p + "skip": _normal(next(ks), (D_INNER,), 1.0),
        p + "w_out": _normal(next(ks), (D_INNER, D_MODEL), D_INNER ** -0.5 * DEEPNORM_BETA),
    }
    d.update(_ln_params(p, ks))
    return d


def _ssd_params(p, ks):
    log_dt = jax.random.uniform(next(ks), (2, SSD_HEADS), jnp.float32,
                                math.log(SSD_DT_MIN), math.log(SSD_DT_MAX))
    dt = jnp.exp(log_dt)
    d = {
        p + "w_in": _normal(next(ks), (D_MODEL, SSD_IN_DIM), D_MODEL ** -0.5),
        p + "conv_w": _normal(next(ks), (SSD_CONV, SSD_CONV_DIM), SSD_CONV ** -0.5),
        p + "conv_b": _normal(next(ks), (SSD_CONV_DIM,), 0.02),
        p + "dt_bias": dt + jnp.log(-jnp.expm1(-dt)),
        p + "a_log": jnp.log(jax.random.uniform(next(ks), (2, SSD_HEADS), jnp.float32, 1.0, 16.0)),
        p + "d_skip": 1.0 + _normal(next(ks), (SSD_HEADS,), 0.1),
        p + "norm_g": 1.0 + _normal(next(ks), (D_INNER,), 0.02),
        p + "w_out": _normal(next(ks), (D_INNER, D_MODEL), D_INNER ** -0.5 * DEEPNORM_BETA),
    }
    d.update(_ln_params(p, ks))
    return d


def _hgrn2_params(p, ks):
    d = {
        p + "w_in": _normal(next(ks), (D_MODEL, HG_IN_DIM), D_MODEL ** -0.5),
        p + "norm_g": 1.0 + _normal(next(ks), (D_INNER,), 0.02),
        p + "w_out": _normal(next(ks), (D_INNER, D_MODEL), D_INNER ** -0.5 * DEEPNORM_BETA),
    }
    d.update(_ln_params(p, ks))
    return d


def setup_inputs(seed: int = 0) -> dict:
    key = jax.random.key(seed)
    ks = iter(jax.random.split(key, 64))
    inputs = {
        "x": _normal(next(ks), (BATCH, SEQ, D_MODEL), 1.0),
        "hgrn_lower_bounds": _normal(next(ks), (DEPTH, 2 * HG_FDIM), 0.1),
    }
    inputs.update(_hyena_params("l0_", ks))
    inputs.update(_ssd_params("l1_", ks))
    inputs.update(_hgrn2_params("l2_", ks))
    inputs.update(_hyena_params("l3_", ks))
    return inputs


def reference(x, hgrn_lower_bounds,
              l0_w_in, l0_conv_w, l0_conv_b, l0_filt_w1, l0_filt_b1, l0_filt_w2, l0_filt_b2,
              l0_filt_w3, l0_filt_b3, l0_filt_freq, l0_filt_w_out, l0_skip, l0_w_out, l0_ln_g, l0_ln_b,
              l1_w_in, l1_conv_w, l1_conv_b, l1_dt_bias, l1_a_log, l1_d_skip, l1_norm_g, l1_w_out,
              l1_ln_g, l1_ln_b,
              l2_w_in, l2_norm_g, l2_w_out, l2_ln_g, l2_ln_b,
              l3_w_in, l3_conv_w, l3_conv_b, l3_filt_w1, l3_filt_b1, l3_filt_w2, l3_filt_b2,
              l3_filt_w3, l3_filt_b3, l3_filt_freq, l3_filt_w_out, l3_skip, l3_w_out, l3_ln_g, l3_ln_b):
    layer_params = [
        (l0_w_in, l0_conv_w, l0_conv_b, l0_filt_w1, l0_filt_b1, l0_filt_w2, l0_filt_b2,
         l0_filt_w3, l0_filt_b3, l0_filt_freq, l0_filt_w_out, l0_skip, l0_w_out),
        (l1_w_in, l1_conv_w, l1_conv_b, l1_dt_bias, l1_a_log, l1_d_skip, l1_norm_g, l1_w_out),
        (l2_w_in, l2_norm_g, l2_w_out),
        (l3_w_in, l3_conv_w, l3_conv_b, l3_filt_w1, l3_filt_b1, l3_filt_w2, l3_filt_b2,
         l3_filt_w3, l3_filt_b3, l3_filt_freq, l3_filt_w_out, l3_skip, l3_w_out),
    ]
    ln_params = [(l0_ln_g, l0_ln_b), (l1_ln_g, l1_ln_b), (l2_ln_g, l2_ln_b), (l3_ln_g, l3_ln_b)]
    lb_all = jnp.cumsum(jax.nn.softmax(hgrn_lower_bounds.astype(jnp.float32), axis=0), axis=0)
    lb_all = lb_all - lb_all[0]
    h = x
    for i in range(DEPTH):
        kind = i % N_MIXERS
        if kind == 0:
            y = hyena_mixer(h, *layer_params[i])
        elif kind == 1:
            y = ssd_mixer(h, *layer_params[i])
        else:
            w_in, norm_g, w_out = layer_params[i]
            y = hgrn2_mixer(h, w_in, lb_all[i], norm_g, w_out)
        h = layer_norm(DEEPNORM_ALPHA * h + y, *ln_params[i]).astype(x.dtype)
    return h
```

```python
import functools
import math

import jax
import jax.numpy as jnp
import numpy as np
from jax import lax
from jax.experimental import pallas as pl
from jax.experimental.pallas import tpu as pltpu

F32 = jnp.float32
BF16 = jnp.bfloat16

D_MODEL = 1024
DEPTH = 4
D_INNER = 2 * D_MODEL
LN_EPS = 1e-5
DEEPNORM_ALPHA = (2.0 * DEPTH) ** 0.25

HG_HEADS = 16
HG_KEY_DIM = 128
HG_FDIM = HG_HEADS * HG_KEY_DIM
HG_CHUNK = 64

V7X_VMEM_LIMIT_BYTES = 56 * 1024 * 1024
LANES = 128
SUBLANES = 8


def _params(*semantics):
    return pltpu.CompilerParams(dimension_semantics=semantics,
                                vmem_limit_bytes=V7X_VMEM_LIMIT_BYTES)


def _dot(a, b):
    return jnp.dot(a.astype(BF16), b.astype(BF16), preferred_element_type=F32)


def _dot_nt(a, b):
    return lax.dot_general(a.astype(BF16), b.astype(BF16), (((1,), (1,)), ((), ())),
                           preferred_element_type=F32)


def _dot_tn(a, b):
    return lax.dot_general(a.astype(BF16), b.astype(BF16), (((0,), (0,)), ((), ())),
                           preferred_element_type=F32)


def _split3(x):
    hi = x.astype(BF16)
    r = x - hi.astype(F32)
    mid = r.astype(BF16)
    lo = (r - mid.astype(F32)).astype(BF16)
    return hi, mid, lo


def _tri_rows(t):
    r = lax.broadcasted_iota(jnp.int32, (t, t), 0)
    c = lax.broadcasted_iota(jnp.int32, (t, t), 1)
    return jnp.where(c <= r, 1.0, 0.0).astype(BF16)


def _cumsum_rows(x, tri):
    w = x.shape[1]
    parts = jnp.concatenate(_split3(x), axis=1)
    y = jnp.dot(tri, parts, preferred_element_type=F32)
    return y[:, :w] + y[:, w:2 * w] + y[:, 2 * w:]


def _sigmoid(x):
    return 1.0 / (1.0 + jnp.exp(-x))


def _silu(x):
    return x * _sigmoid(x)


def _matmul_kernel(x_ref, w_ref, o_ref):
    o_ref[...] = _dot(x_ref[...], w_ref[...])


def _matmul(x, w, *, tm, tn):
    m, k = x.shape
    n = w.shape[1]
    tm = min(tm, m)
    return pl.pallas_call(
        _matmul_kernel,
        out_shape=jax.ShapeDtypeStruct((m, n), F32),
        grid=(m // tm, n // tn),
        in_specs=[pl.BlockSpec((tm, k), lambda i, j: (i, 0)),
                  pl.BlockSpec((k, tn), lambda i, j: (0, j))],
        out_specs=pl.BlockSpec((tm, tn), lambda i, j: (i, j)),
        compiler_params=_params("parallel", "arbitrary"),
        name="in_proj",
    )(x, w)


def _residual_ln(y, w_ref, h_ref, g_ref, b_ref, o_ref):
    r = DEEPNORM_ALPHA * h_ref[...] + _dot(y, w_ref[...])
    mu = jnp.mean(r, axis=-1, keepdims=True)
    d = r - mu
    var = jnp.mean(d * d, axis=-1, keepdims=True)
    o_ref[...] = d * lax.rsqrt(var + LN_EPS) * g_ref[...] + b_ref[...]


def _outproj_ln_kernel(y_ref, w_ref, h_ref, g_ref, b_ref, o_ref):
    _residual_ln(y_ref[...], w_ref, h_ref, g_ref, b_ref, o_ref)


def _groupnorm_outproj_ln_kernel(y_ref, ng_ref, w_ref, h_ref, g_ref, b_ref, o_ref, *, n_groups):
    y = y_ref[...]
    gw = y.shape[1] // n_groups
    parts = []
    for i in range(n_groups):
        yi = y[:, i * gw:(i + 1) * gw]
        ms = jnp.mean(yi * yi, axis=-1, keepdims=True)
        parts.append(yi * lax.rsqrt(ms + LN_EPS))
    y = jnp.concatenate(parts, axis=1) * ng_ref[...]
    _residual_ln(y, w_ref, h_ref, g_ref, b_ref, o_ref)


def _outproj_ln(y, w, h, ln_g, ln_b, *, tm, norm_g=None, n_groups=0):
    m, k = y.shape
    n = w.shape[1]
    tm = min(tm, m)
    row = lambda width: pl.BlockSpec((1, width), lambda i: (0, 0))
    in_specs = [pl.BlockSpec((tm, k), lambda i: (i, 0))]
    args = [y]
    if n_groups:
        body = functools.partial(_groupnorm_outproj_ln_kernel, n_groups=n_groups)
        in_specs.append(row(k))
        args.append(norm_g.reshape(1, k))
    else:
        body = _outproj_ln_kernel
    in_specs += [pl.BlockSpec((k, n), lambda i: (0, 0)), pl.BlockSpec((tm, n), lambda i: (i, 0)),
                 row(n), row(n)]
    args += [w, h, ln_g.reshape(1, n), ln_b.reshape(1, n)]
    return pl.pallas_call(
        body,
        out_shape=jax.ShapeDtypeStruct((m, n), F32),
        grid=(m // tm,),
        in_specs=in_specs,
        out_specs=pl.BlockSpec((tm, n), lambda i: (i, 0)),
        compiler_params=_params("parallel"),
        name="out_proj_ln",
    )(*args)


def _segment_ref_rows(a, seg, j):
    t, w = a.shape
    if seg >= SUBLANES:
        a3 = a.reshape(t // seg, seg, w)
        return jnp.broadcast_to(a3[:, j:j + 1, :], (t // seg, seg, w)).reshape(t, w)
    a3 = a.reshape(t // SUBLANES, SUBLANES, w)
    sub = lax.broadcasted_iota(jnp.int32, a3.shape, 1)
    out = jnp.zeros_like(a3)
    for s0 in range(0, SUBLANES, seg):
        row = jnp.broadcast_to(a3[:, s0 + j:s0 + j + 1, :], a3.shape)
        out = jnp.where((sub >= s0) & (sub < s0 + seg), row, out)
    return out.reshape(t, w)


def _gla_intra(q, k, a, reverse):
    t = q.shape[0]
    row = lax.broadcasted_iota(jnp.int32, (t, t), 0)
    col = lax.broadcasted_iota(jnp.int32, (t, t), 1)
    pos = lax.broadcasted_iota(jnp.int32, q.shape, 0)
    att = jnp.where(row == col, jnp.sum(q * k, axis=1, keepdims=True), 0.0)
    m = t // 2
    while m >= 1:
        seg = 2 * m
        upper = (pos % seg) >= m
        if not reverse:
            ref = _segment_ref_rows(a, seg, m - 1)
            qt = jnp.where(upper, q * jnp.exp(a - ref), 0.0)
            ks = jnp.where(upper, 0.0, k * jnp.exp(ref - a))
            pair = ((row % seg) >= m) & ((col % seg) < m)
        else:
            ref = _segment_ref_rows(a, seg, m)
            qt = jnp.where(upper, 0.0, q * jnp.exp(ref - a))
            ks = jnp.where(upper, k * jnp.exp(a - ref), 0.0)
            pair = ((row % seg) < m) & ((col % seg) >= m)
        same = (row // seg) == (col // seg)
        att = att + jnp.where(same & pair, _dot_nt(qt, ks), 0.0)
        m //= 2
    return att


def _hgrn2_kernel(lbf_ref, lbb_ref, q_ref, ff_ref, fb_ref, v_ref, z_ref, g_ref, o_ref,
                  st_ref, *, layer, chunk):
    seq = q_ref.shape[0]
    nc = seq // chunk
    tri = _tri_rows(chunk)

    def lower_bound(ref):
        p = ref[...]
        e = jnp.exp(p - jnp.max(p, axis=0, keepdims=True))
        s = e / jnp.sum(e, axis=0, keepdims=True)
        lb = jnp.zeros_like(s[0:1])
        for j in range(1, layer + 1):
            lb = lb + s[j:j + 1]
        return lb

    def gates(fr, lb):
        log_sig = jnp.minimum(fr, 0.0) - jnp.log1p(jnp.exp(-jnp.abs(fr)))
        a = jnp.log(lb)
        b = jnp.log1p(-lb) + log_sig
        log_f = jnp.maximum(a, b) + jnp.log1p(jnp.exp(-jnp.abs(a - b)))
        key = (1.0 - lb) * _sigmoid(-fr)
        return log_f, key

    def run(f_ref, lb, reverse):
        st_ref[...] = jnp.zeros_like(st_ref)

        def body(i, carry):
            c = (nc - 1 - i) if reverse else i
            rows = pl.ds(pl.multiple_of(c * chunk, chunk), chunk)
            q = q_ref[rows, :]
            v = v_ref[rows, :]
            log_f, key = gates(f_ref[rows, :], lb)
            g = _cumsum_rows(log_f, tri)
            g_last = g[chunk - 1:chunk, :]
            st = st_ref[...]
            if not reverse:
                att = _gla_intra(q, key, g, False)
                q_in = q * jnp.exp(g)
                k_out = key * jnp.exp(g_last - g)
            else:
                e = g - log_f
                att = _gla_intra(q, key, e, True)
                q_in = q * jnp.exp(g_last - e)
                k_out = key * jnp.exp(e)
            o = _dot_nt(q_in, st) + _dot(att, v)
            st_ref[...] = st * jnp.exp(g_last) + _dot_tn(v, k_out)
            if not reverse:
                o_ref[rows, :] = o
            else:
                o = o + o_ref[rows, :]
                ms = jnp.mean(o * o, axis=-1, keepdims=True)
                o_ref[rows, :] = o * lax.rsqrt(ms + LN_EPS) * g_ref[...] * _silu(z_ref[rows, :])
            return carry

        lax.fori_loop(0, nc, body, 0)

    run(ff_ref, lower_bound(lbf_ref), False)
    run(fb_ref, lower_bound(lbb_ref), True)


def _hgrn2_core(proj, lower_bounds, norm_g, *, layer, batch, seq):
    hb = HG_FDIM // HG_KEY_DIM
    col = lambda base: pl.BlockSpec((None, seq, HG_KEY_DIM), lambda b, h: (b, 0, base * hb + h))
    lb_spec = lambda base: pl.BlockSpec((DEPTH, HG_KEY_DIM), lambda b, h: (0, base * hb + h))
    return pl.pallas_call(
        functools.partial(_hgrn2_kernel, layer=layer, chunk=HG_CHUNK),
        out_shape=jax.ShapeDtypeStruct((batch, seq, D_INNER), F32),
        grid=(batch, HG_HEADS),
        in_specs=[lb_spec(0), lb_spec(1), col(0), col(1), col(2), col(3), col(4),
                  pl.BlockSpec((1, HG_KEY_DIM), lambda b, h: (0, h))],
        out_specs=pl.BlockSpec((None, seq, HG_KEY_DIM), lambda b, h: (b, 0, h)),
        scratch_shapes=[pltpu.VMEM((HG_KEY_DIM, HG_KEY_DIM), F32)],
        compiler_params=_params("parallel", "parallel"),
        name="hgrn2_core",
    )(lower_bounds, lower_bounds, proj, proj, proj, proj, proj, norm_g.reshape(1, D_INNER))


def _hgrn2_layer(h, lower_bounds, w_in, norm_g, w_out, ln_g, ln_b, *, layer):
    batch, seq, d = h.shape
    h2 = h.reshape(batch * seq, d)
    proj = _matmul(h2, w_in.astype(BF16), tm=1024, tn=1024)
    o = _hgrn2_core(proj.reshape(batch, seq, -1), lower_bounds, norm_g,
                    layer=layer, batch=batch, seq=seq)
    out = _outproj_ln(o.reshape(batch * seq, D_INNER), w_out.astype(BF16), h2, ln_g, ln_b, tm=512)
    return out.reshape(batch, seq, d)


SSD_HEADDIM = 64
SSD_HEADS = D_INNER // SSD_HEADDIM
SSD_STATE = 128
SSD_GROUPS = 4
SSD_CONV = 5
SSD_CHUNK = 128
SSD_PAIRS = SSD_HEADS // 2
SSD_PAIRS_PER_GROUP = SSD_PAIRS // SSD_GROUPS
CONV_HALO = SUBLANES


def _softplus(x):
    return jnp.maximum(x, 0.0) + jnp.log1p(jnp.exp(-jnp.abs(x)))


def _ssd_kernel(z_ref, x_ref, b_ref, c_ref, dt_ref, wx_ref, wb_ref, wc_ref, bx_ref, bb_ref, bc_ref,
                dtb_ref, alog_ref, dskip_ref, o_ref,
                xc_s, bc_s, cc_s, dt_s, cs_s, cst_s, adtt_s, win_s, st_s, *, chunk):
    seq = x_ref.shape[0]
    nc = seq // chunk
    pair = pl.program_id(1)
    tri = _tri_rows(chunk)
    lane = lax.broadcasted_iota(jnp.int32, (1, LANES), 1)
    first_head = lane < SSD_HEADDIM
    row = lax.broadcasted_iota(jnp.int32, (chunk, chunk), 0)
    col = lax.broadcasted_iota(jnp.int32, (chunk, chunk), 1)

    def chunk_rows(c):
        return pl.ds(pl.multiple_of(c * chunk, chunk), chunk)

    def head_row(t_ref, head, cols):
        base = pl.multiple_of((head // SUBLANES) * SUBLANES, SUBLANES)
        blk = t_ref[pl.ds(base, SUBLANES), cols]
        sub = lax.broadcasted_iota(jnp.int32, blk.shape, 0)
        return jnp.sum(jnp.where(sub == head - base, blk, 0.0), axis=0, keepdims=True)

    def conv_silu(src_ref, w_ref, bias_ref, c):
        r0 = pl.multiple_of(c * chunk, chunk)
        prev = src_ref[pl.ds(pl.multiple_of(jnp.maximum(r0 - CONV_HALO, 0), CONV_HALO), CONV_HALO), :]
        nxt = src_ref[pl.ds(pl.multiple_of(jnp.minimum(r0 + chunk, seq - CONV_HALO), CONV_HALO), CONV_HALO), :]
        win_s[0:CONV_HALO, :] = jnp.where(c > 0, prev, 0.0)
        win_s[CONV_HALO:CONV_HALO + chunk, :] = src_ref[pl.ds(r0, chunk), :]
        win_s[CONV_HALO + chunk:, :] = jnp.where(c < nc - 1, nxt, 0.0)
        acc = jnp.zeros((chunk, LANES), F32) + bias_ref[...]
        for j in range(SSD_CONV):
            acc = acc + win_s[pl.ds(CONV_HALO - SSD_CONV // 2 + j, chunk), :] * w_ref[j:j + 1, :]
        return _silu(acc)

    a_neg = -jnp.exp(alog_ref[...])

    def prologue(c, carry):
        rows = chunk_rows(c)
        xc_s[rows, :] = conv_silu(x_ref, wx_ref, bx_ref, c)
        bc_s[rows, :] = conv_silu(b_ref, wb_ref, bb_ref, c)
        cc_s[rows, :] = conv_silu(c_ref, wc_ref, bc_ref, c)
        dt = _softplus(dt_ref[rows, :] + dtb_ref[...])
        adt = dt * a_neg
        cs = _cumsum_rows(adt, tri)
        dt_s[rows, :] = dt
        cs_s[rows, :] = cs
        cst_s[:, rows] = cs.T
        adtt_s[:, rows] = adt.T
        return carry

    lax.fori_loop(0, nc, prologue, 0)

    def run(reverse):
        st_s[...] = jnp.zeros_like(st_s)

        def body(i, carry):
            c = (nc - 1 - i) if reverse else i
            rows = chunk_rows(c)
            xc = xc_s[rows, :]
            bc = bc_s[rows, :]
            cc = cc_s[rows, :]
            dt = dt_s[rows, :]
            cs = cs_s[rows, :]
            cb = _dot_nt(cc, bc)
            y = jnp.zeros((chunk, LANES), F32)
            dt_cols, a_cols, lasts = [], [], []
            for j in range(2):
                head = (SSD_HEADS if reverse else 0) + 2 * pair + j
                onehot = lane == head
                dt_col = jnp.sum(jnp.where(onehot, dt, 0.0), axis=1, keepdims=True)
                cs_col = jnp.sum(jnp.where(onehot, cs, 0.0), axis=1, keepdims=True)
                cs_row = head_row(cst_s, head, rows)
                last = cs_col[chunk - 1:chunk, :]
                if not reverse:
                    a_col, a_row = cs_col, cs_row
                    decay = jnp.where(col <= row, a_col - a_row, -jnp.inf)
                else:
                    a_col = cs_col - dt_col * jnp.sum(jnp.where(onehot, a_neg, 0.0), axis=1, keepdims=True)
                    a_row = cs_row - head_row(adtt_s, head, rows)
                    decay = jnp.where(col >= row, a_row - a_col, -jnp.inf)
                head_lanes = first_head if j == 0 else jnp.logical_not(first_head)
                xdt = jnp.where(head_lanes, xc * dt_col, 0.0)
                y = y + _dot(cb * jnp.exp(decay), xdt)
                dt_cols.append(dt_col)
                a_cols.append(a_col)
                lasts.append(last)
            per_head = lambda vals: jnp.where(first_head, vals[0], vals[1])
            dt_l, a_l, last_l = per_head(dt_cols), per_head(a_cols), per_head(lasts)
            st = st_s[...]
            if not reverse:
                y = y + _dot(cc, st) * jnp.exp(a_l)
                st_in = xc * dt_l * jnp.exp(last_l - a_l)
            else:
                y = y + _dot(cc, st) * jnp.exp(last_l - a_l)
                st_in = xc * dt_l * jnp.exp(a_l)
            st_s[...] = st * jnp.exp(last_l) + _dot_tn(bc, st_in)
            if not reverse:
                o_ref[rows, :] = y
            else:
                y = y + o_ref[rows, :] + xc * dskip_ref[...]
                o_ref[rows, :] = y * _silu(z_ref[rows, :])
            return carry

        lax.fori_loop(0, nc, body, 0)

    run(False)
    run(True)


def _ssd_core(proj, dt_raw, conv_w, conv_b, dt_bias, a_log, d_skip, *, batch, seq):
    xb = D_INNER // LANES
    gn = SSD_GROUPS * SSD_STATE // LANES
    ppg = SSD_PAIRS_PER_GROUP
    seq_blk = lambda f: pl.BlockSpec((None, seq, LANES), lambda b, p: (b, 0, f(p)))
    par_blk = lambda rows, f: pl.BlockSpec((rows, LANES), lambda b, p: (0, f(p)))
    pad = LANES - 2 * SSD_HEADS
    dtb = jnp.pad(dt_bias.reshape(1, 2 * SSD_HEADS), ((0, 0), (0, pad)))
    alog = jnp.pad(a_log.reshape(1, 2 * SSD_HEADS), ((0, 0), (0, pad)))
    dskip = jnp.repeat(d_skip, SSD_HEADDIM).reshape(1, D_INNER)
    cw = conv_w
    cbias = conv_b.reshape(1, -1)
    scratch = [pltpu.VMEM((seq, LANES), F32)] * 5 + [pltpu.VMEM((LANES, seq), F32)] * 2 + [
        pltpu.VMEM((SSD_CHUNK + 2 * CONV_HALO, LANES), F32), pltpu.VMEM((SSD_STATE, LANES), F32)]
    return pl.pallas_call(
        functools.partial(_ssd_kernel, chunk=SSD_CHUNK),
        out_shape=jax.ShapeDtypeStruct((batch, seq, D_INNER), F32),
        grid=(batch, SSD_PAIRS),
        in_specs=[seq_blk(lambda p: p), seq_blk(lambda p: xb + p),
                  seq_blk(lambda p: 2 * xb + p // ppg), seq_blk(lambda p: 2 * xb + gn + p // ppg),
                  pl.BlockSpec((None, seq, LANES), lambda b, p: (b, 0, 0)),
                  par_blk(SSD_CONV, lambda p: p), par_blk(SSD_CONV, lambda p: xb + p // ppg),
                  par_blk(SSD_CONV, lambda p: xb + gn + p // ppg),
                  par_blk(1, lambda p: p), par_blk(1, lambda p: xb + p // ppg),
                  par_blk(1, lambda p: xb + gn + p // ppg),
                  par_blk(1, lambda p: 0), par_blk(1, lambda p: 0), par_blk(1, lambda p: p)],
        out_specs=pl.BlockSpec((None, seq, LANES), lambda b, p: (b, 0, p)),
        scratch_shapes=scratch,
        compiler_params=_params("parallel", "parallel"),
        name="ssd_core",
    )(proj, proj, proj, proj, dt_raw, cw, cw, cw, cbias, cbias, cbias, dtb, alog, dskip)


def _ssd_layer(h, w_in, conv_w, conv_b, dt_bias, a_log, d_skip, norm_g, w_out, ln_g, ln_b):
    batch, seq, d = h.shape
    h2 = h.reshape(batch * seq, d)
    n_main = 2 * D_INNER + 2 * SSD_GROUPS * SSD_STATE
    w_main = w_in[:, :n_main].astype(BF16)
    w_dt = jnp.pad(w_in[:, n_main:], ((0, 0), (0, LANES - 2 * SSD_HEADS))).astype(BF16)
    proj = _matmul(h2, w_main, tm=1024, tn=1024)
    dt_raw = _matmul(h2, w_dt, tm=1024, tn=LANES)
    y = _ssd_core(proj.reshape(batch, seq, n_main), dt_raw.reshape(batch, seq, LANES),
                  conv_w, conv_b, dt_bias, a_log, d_skip, batch=batch, seq=seq)
    out = _outproj_ln(y.reshape(batch * seq, D_INNER), w_out.astype(BF16), h2, ln_g, ln_b, tm=512,
                      norm_g=norm_g, n_groups=SSD_GROUPS)
    return out.reshape(batch, seq, d)


HY_SHORT_CONV = 3
HY_EMB_DIM = 33
HY_FILTER_HIDDEN = 64
HY_FAST_DECAY_PCT = 0.3
HY_SLOW_DECAY_PCT = 1.5
HY_DECAY_TARGET = 1e-2
HY_TIME_BLOCKS = 16
HY_NK = HY_TIME_BLOCKS // 2 + 1
HY_CB = 256
HY_TL = 1024


def _dft_tables(seq):
    n = 2 * seq
    bp = n // HY_TIME_BLOCKS
    b = np.arange(bp, dtype=np.int64)[:, None]
    kb = np.arange(bp, dtype=np.int64)[None, :]
    tabs = []
    for ka in range(HY_NK):
        th = 2.0 * np.pi * ((b * (ka + HY_TIME_BLOCKS * kb)) % n) / n
        c, s = np.cos(th), np.sin(th)
        tabs.append(np.block([[c, -s], [s, c]]))
    return jnp.asarray(np.stack(tabs), dtype=BF16)


def _dft_coefficients(seq):
    a = np.arange(HY_TIME_BLOCKS)[None, :]
    ka = np.arange(HY_NK)[:, None]
    th = 2.0 * np.pi * a * ka / HY_TIME_BLOCKS
    wk = np.where((ka == 0) | (ka == HY_TIME_BLOCKS // 2), 1.0, 2.0) / (2 * seq)
    return jnp.asarray(np.concatenate([np.cos(th), -np.sin(th), wk * np.cos(th), -wk * np.sin(th)]),
                       dtype=F32)


def _dft_forward_block(coef_ref, src_ref, t_ref, ka, n_src_blocks):
    bp = t_ref.shape[0] // 2
    yre = yim = None
    for a in range(n_src_blocks):
        blk = src_ref[:, a * bp:(a + 1) * bp]
        tre = coef_ref[ka, a] * blk
        tim = coef_ref[HY_NK + ka, a] * blk
        yre = tre if yre is None else yre + tre
        yim = tim if yim is None else yim + tim
    return _dot(jnp.concatenate([yre, yim], axis=1), t_ref[...])


def _dot_f32(a, b):
    return jnp.dot(a, b, precision=lax.Precision.HIGHEST, preferred_element_type=F32)


def _hyena_mlp_kernel(z_ref, w1_ref, b1_ref, w2_ref, b2_ref, w3_ref, b3_ref, fr_ref, o_ref):
    fr = fr_ref[...]
    h = jnp.sin(fr * (_dot_f32(w1_ref[...], z_ref[...]) + b1_ref[...]))
    h = jnp.sin(fr * (_dot_f32(w2_ref[...], h) + b2_ref[...]))
    o_ref[...] = jnp.sin(fr * (_dot_f32(w3_ref[...], h) + b3_ref[...]))


def _hyena_filter_kernel(coef_ref, h_ref, wf_ref, wb_ref, t_row_ref, delta_ref, t_ref, o_ref, k_s):
    ka = pl.program_id(1)
    seq = k_s.shape[1] // 2

    @pl.when(ka == 0)
    def _():
        decay = jnp.exp(-t_row_ref[...] * delta_ref[...])
        k_s[:, :seq] = _dot_f32(wf_ref[...], h_ref[:, :seq]) * decay[:, :seq]
        lane = lax.broadcasted_iota(jnp.int32, (1, seq), 1)
        k_s[:, seq:] = jnp.where(lane == 0, 0.0, _dot_f32(wb_ref[...], h_ref[:, seq:]) * decay[:, seq:])

    o_ref[...] = _dft_forward_block(coef_ref, k_s, t_ref, ka, HY_TIME_BLOCKS)


def _hyena_in_kernel(x_ref, w0_ref, w1_ref, w2_ref, w3_ref, cw0_ref, cw1_ref, cw2_ref,
                     cb0_ref, cb1_ref, cb2_ref, p_ref, g_ref, *, tl):
    seq = x_ref.shape[0]
    halo = SUBLANES
    for i in range(seq // tl):
        lo, hi = max(i * tl - halo, 0), min((i + 1) * tl + halo, seq)
        off, rows = i * tl - lo, hi - lo
        xs = x_ref[lo:hi, :]
        pos = lax.broadcasted_iota(jnp.int32, (rows, 1), 0) + lo

        def short_conv(w_ref, cw_ref, cb_ref):
            u = _dot(xs, w_ref[...])
            prev = jnp.where(pos == 0, 0.0, pltpu.roll(u, 1, 0))
            nxt = jnp.where(pos == seq - 1, 0.0, pltpu.roll(u, rows - 1, 0))
            c = cw_ref[0:1, :] * prev + cw_ref[1:2, :] * u + cw_ref[2:3, :] * nxt + cb_ref[...]
            return c[off:off + tl]

        x0 = short_conv(w0_ref, cw0_ref, cb0_ref)
        x1 = short_conv(w1_ref, cw1_ref, cb1_ref)
        v = short_conv(w2_ref, cw2_ref, cb2_ref)
        z = _dot(xs, w3_ref[...])[off:off + tl]
        p_ref[:, i * tl:(i + 1) * tl] = (x1 * v).T
        g_ref[:, i * tl:(i + 1) * tl] = (x0 * _silu(z)).T


def _hyena_conv_kernel(coef_ref, p_ref, g_ref, skip_ref, kf_ref, t_ref, o_ref):
    ka = pl.program_id(2)
    bp = t_ref.shape[0] // 2
    half = HY_TIME_BLOCKS // 2
    x = _dft_forward_block(coef_ref, p_ref, t_ref, ka, half)
    kf = kf_ref[...]
    xre, xim, kre, kim = x[:, :bp], x[:, bp:], kf[:, :bp], kf[:, bp:]
    z = jnp.concatenate([xre * kre - xim * kim, xre * kim + xim * kre], axis=1)
    g = _dot_nt(z, t_ref[...])
    gre, gim = g[:, :bp], g[:, bp:]

    @pl.when(ka == 0)
    def _():
        o_ref[...] = jnp.zeros_like(o_ref)

    for a in range(half):
        cols = slice(a * bp, (a + 1) * bp)
        o_ref[:, cols] += coef_ref[2 * HY_NK + ka, a] * gre + coef_ref[3 * HY_NK + ka, a] * gim

    @pl.when(ka == HY_NK - 1)
    def _():
        o_ref[...] = g_ref[...] * (o_ref[...] + skip_ref[...] * p_ref[...])


def _hyena_out_kernel(y_ref, w_ref, h_ref, g_ref, b_ref, o_ref):
    r = DEEPNORM_ALPHA * h_ref[...] + _dot_tn(y_ref[...], w_ref[...])
    mu = jnp.mean(r, axis=-1, keepdims=True)
    d = r - mu
    var = jnp.mean(d * d, axis=-1, keepdims=True)
    o_ref[...] = d * lax.rsqrt(var + LN_EPS) * g_ref[...] + b_ref[...]


def _hyena_positions(seq):
    pos = jnp.concatenate([jnp.arange(seq), jnp.zeros((1,), jnp.int32), seq - jnp.arange(1, seq)])
    t = jnp.linspace(0.0, 1.0, seq, dtype=F32)[:, None]
    bands = (HY_EMB_DIM - 1) // 2
    w = 2.0 * math.pi * jnp.arange(seq, dtype=F32)[:, None] / seq
    f = jnp.linspace(1e-4, bands - 1, bands, dtype=F32)[None, :]
    z = jnp.concatenate([t, jnp.cos(f * w), -jnp.sin(f * w)], axis=-1)
    zt = jnp.pad(z[pos].T, ((0, HY_FILTER_HIDDEN - HY_EMB_DIM), (0, 0)))
    return zt, t[pos].T


def _hyena_filter_spectrum(seq, tables, coef, w1, b1, w2, b2, w3, b3, freq, w_out):
    hid = HY_FILTER_HIDDEN
    n = 2 * seq
    bp2 = tables.shape[1]
    zt, t_row = _hyena_positions(seq)
    col = lambda v: v.astype(F32).reshape(hid, 1)
    w1t = jnp.pad(w1.astype(F32).T, ((0, 0), (0, hid - HY_EMB_DIM)))
    h3 = pl.pallas_call(
        _hyena_mlp_kernel,
        out_shape=jax.ShapeDtypeStruct((hid, n), F32),
        compiler_params=_params(),
        name="hyena_filter_mlp",
    )(zt, w1t, col(b1), w2.astype(F32).T, col(b2), w3.astype(F32).T, col(b3), col(freq))
    min_d = math.log(HY_DECAY_TARGET) / HY_FAST_DECAY_PCT
    max_d = math.log(HY_DECAY_TARGET) / HY_SLOW_DECAY_PCT
    delta = jnp.abs(jnp.linspace(min_d, max_d, D_INNER, dtype=F32)).reshape(D_INNER, 1)
    wt = w_out.astype(F32).T
    cblks = D_INNER // HY_CB
    return pl.pallas_call(
        _hyena_filter_kernel,
        out_shape=jax.ShapeDtypeStruct((HY_NK, D_INNER, bp2), F32),
        grid=(cblks, HY_NK),
        in_specs=[pl.BlockSpec(memory_space=pltpu.SMEM),
                  pl.BlockSpec((hid, n), lambda c, k: (0, 0)),
                  pl.BlockSpec((HY_CB, hid), lambda c, k: (c, 0)),
                  pl.BlockSpec((HY_CB, hid), lambda c, k: (cblks + c, 0)),
                  pl.BlockSpec((1, n), lambda c, k: (0, 0)),
                  pl.BlockSpec((HY_CB, 1), lambda c, k: (c, 0)),
                  pl.BlockSpec((None, bp2, bp2), lambda c, k: (k, 0, 0))],
        out_specs=pl.BlockSpec((None, HY_CB, bp2), lambda c, k: (k, c, 0)),
        scratch_shapes=[pltpu.VMEM((HY_CB, n), F32)],
        compiler_params=_params("parallel", "arbitrary"),
        name="hyena_filter_spectrum",
    )(coef, h3, wt, wt, t_row, delta, tables)


def _hyena_layer(h, w_in, conv_w, conv_b, w1, b1, w2, b2, w3, b3, freq, filt_w_out, skip, w_out,
                 ln_g, ln_b):
    batch, seq, d = h.shape
    tables = _dft_tables(seq)
    coef = _dft_coefficients(seq)
    bp2 = tables.shape[1]
    kf = _hyena_filter_spectrum(seq, tables, coef, w1, b1, w2, b2, w3, b3, freq, filt_w_out)

    cblks = D_INNER // HY_CB
    tl = min(HY_TL, seq)
    w_blk = lambda g: pl.BlockSpec((d, HY_CB), lambda b, c: (0, g * cblks + c))
    cw_blk = lambda g: pl.BlockSpec((HY_SHORT_CONV, HY_CB), lambda b, c: (0, g * cblks + c))
    cb_blk = lambda g: pl.BlockSpec((1, HY_CB), lambda b, c: (0, g * cblks + c))
    chan_seq = jax.ShapeDtypeStruct((batch, D_INNER, seq), F32)
    w_bf = w_in.astype(BF16)
    cb2 = conv_b.reshape(1, -1)
    p, g = pl.pallas_call(
        functools.partial(_hyena_in_kernel, tl=tl),
        out_shape=(chan_seq, chan_seq),
        grid=(batch, cblks),
        in_specs=[pl.BlockSpec((None, seq, d), lambda b, c: (b, 0, 0)),
                  w_blk(0), w_blk(1), w_blk(2), w_blk(3),
                  cw_blk(0), cw_blk(1), cw_blk(2), cb_blk(0), cb_blk(1), cb_blk(2)],
        out_specs=(pl.BlockSpec((None, HY_CB, seq), lambda b, c: (b, c, 0)),
                   pl.BlockSpec((None, HY_CB, seq), lambda b, c: (b, c, 0))),
        compiler_params=_params("parallel", "parallel"),
        name="hyena_in_proj",
    )(h.astype(BF16), w_bf, w_bf, w_bf, w_bf, conv_w, conv_w, conv_w, cb2, cb2, cb2)

    blk = pl.BlockSpec((None, HY_CB, seq), lambda b, c, k: (b, c, 0))
    y = pl.pallas_call(
        _hyena_conv_kernel,
        out_shape=chan_seq,
        grid=(batch, cblks, HY_NK),
        in_specs=[pl.BlockSpec(memory_space=pltpu.SMEM), blk, blk,
                  pl.BlockSpec((HY_CB, 1), lambda b, c, k: (c, 0)),
                  pl.BlockSpec((None, HY_CB, bp2), lambda b, c, k: (k, c, 0)),
                  pl.BlockSpec((None, bp2, bp2), lambda b, c, k: (k, 0, 0))],
        out_specs=blk,
        compiler_params=_params("parallel", "parallel", "arbitrary"),
        name="hyena_long_conv",
    )(coef, p, g, skip.astype(F32).reshape(D_INNER, 1), kf, tables)

    tq = min(512, seq)
    row = pl.BlockSpec((1, d), lambda b, i: (0, 0))
    return pl.pallas_call(
        _hyena_out_kernel,
        out_shape=jax.ShapeDtypeStruct((batch, seq, d), F32),
        grid=(batch, seq // tq),
        in_specs=[pl.BlockSpec((None, D_INNER, tq), lambda b, i: (b, 0, i)),
                  pl.BlockSpec((D_INNER, d), lambda b, i: (0, 0)),
                  pl.BlockSpec((None, tq, d), lambda b, i: (b, i, 0)), row, row],
        out_specs=pl.BlockSpec((None, tq, d), lambda b, i: (b, i, 0)),
        compiler_params=_params("parallel", "parallel"),
        name="hyena_out_proj_ln",
    )(y, w_out.astype(BF16), h, ln_g.reshape(1, d), ln_b.reshape(1, d))


def kernel(x, hgrn_lower_bounds, l0_w_in, l0_conv_w, l0_conv_b, l0_filt_w1, l0_filt_b1, l0_filt_w2, l0_filt_b2, l0_filt_w3, l0_filt_b3, l0_filt_freq, l0_filt_w_out, l0_skip, l0_w_out, l0_ln_g, l0_ln_b, l1_w_in, l1_conv_w, l1_conv_b, l1_dt_bias, l1_a_log, l1_d_skip, l1_norm_g, l1_w_out, l1_ln_g, l1_ln_b, l2_w_in, l2_norm_g, l2_w_out, l2_ln_g, l2_ln_b, l3_w_in, l3_conv_w, l3_conv_b, l3_filt_w1, l3_filt_b1, l3_filt_w2, l3_filt_b2, l3_filt_w3, l3_filt_b3, l3_filt_freq, l3_filt_w_out, l3_skip, l3_w_out, l3_ln_g, l3_ln_b):
    h = x
    h = _hyena_layer(h, l0_w_in, l0_conv_w, l0_conv_b, l0_filt_w1, l0_filt_b1, l0_filt_w2, l0_filt_b2,
                     l0_filt_w3, l0_filt_b3, l0_filt_freq, l0_filt_w_out, l0_skip, l0_w_out, l0_ln_g, l0_ln_b)
    h = _ssd_layer(h, l1_w_in, l1_conv_w, l1_conv_b, l1_dt_bias, l1_a_log, l1_d_skip, l1_norm_g, l1_w_out,
                   l1_ln_g, l1_ln_b)
    h = _hgrn2_layer(h, hgrn_lower_bounds, l2_w_in, l2_norm_g, l2_w_out, l2_ln_g, l2_ln_b, layer=2)
    h = _hyena_layer(h, l3_w_in, l3_conv_w, l3_conv_b, l3_filt_w1, l3_filt_b1, l3_filt_w2, l3_filt_b2,
                     l3_filt_w3, l3_filt_b3, l3_filt_freq, l3_filt_w_out, l3_skip, l3_w_out, l3_ln_g, l3_ln_b)
    return h
```
